```python
import math
import jax, jax.numpy as jnp
from jax import lax
import numpy as np


D_MODEL = 1024
BATCH = 8
SEQ = 2048
DEPTH = 1
DEC_BATCH = 128
DEC_SEQ = 4
PAST_LEN = 16384
PAGE_SIZE = 128

D_MIX = D_MODEL
MIX_A = D_MIX // 2
A_HEADS = 4
A_DK = MIX_A // A_HEADS
A_DV = MIX_A // A_HEADS
MIX_B = D_MIX - MIX_A
CONV_W = 31
D_IN = 4 * MIX_A + 2 * MIX_B
D_FF = ((8 * D_MODEL + 3 * 256 - 1) // (3 * 256)) * 256
D_PLE = 256
CHUNK = 32
EPS = 1e-6

kernel_name = "hymba_hgrn2_conformer_conv_decode_step"


def _rmsnorm(x, g):
    xf = x.astype(jnp.float32)
    return xf * lax.rsqrt(jnp.mean(xf * xf, axis=-1, keepdims=True) + EPS) * g.astype(jnp.float32)


def _hgrn2_scan(q, k, v, logf, s0):
    B, T, H, DK = q.shape
    DV = v.shape[-1]
    C = math.gcd(T, CHUNK)
    n = T // C

    def to_chunks(a):
        return a.reshape(B, n, C, H, a.shape[-1]).transpose(1, 0, 3, 2, 4)

    qc, kc, vc, gc = to_chunks(q), to_chunks(k), to_chunks(v), to_chunks(logf)
    mask = jnp.tril(jnp.ones((C, C), dtype=bool))

    def step(s, inp):
        qi, ki, vi, gi = inp
        b = jnp.cumsum(gi, axis=2)
        b_last = b[:, :, -1:, :]
        q_dec = qi * jnp.exp(b)
        k_inv = ki * jnp.exp(-b)
        o_inter = jnp.einsum('bhtk,bhkv->bhtv', q_dec, s)
        scores = jnp.where(mask, jnp.einsum('bhtk,bhsk->bhts', q_dec, k_inv), 0.0)
        o_intra = jnp.einsum('bhts,bhsv->bhtv', scores, vi)
        s_new = jnp.exp(b_last[:, :, 0, :])[..., None] * s + jnp.einsum(
            'bhsk,bhsv->bhkv', ki * jnp.exp(b_last - b), vi)
        return s_new, o_inter + o_intra

    s_fin, o = lax.scan(step, s0, (qc, kc, vc, gc))
    o = o.transpose(1, 0, 3, 2, 4).reshape(B, T, H, DV)
    return o, s_fin


def _layer(h, p_l, s0, buf0, lb, norm_mix, w_in, hgrn_out_norm, conv_dw, conv_dw_bias,
           conv_ln_gain, conv_ln_bias, w_out, norm_ffn, w_ffn_gate, w_ffn_up, w_ffn_down,
           norm_ple, w_ple_gate, w_ple_proj):
    f32 = jnp.float32
    B, T, _ = h.shape
    xn = _rmsnorm(h, norm_mix)
    proj = xn @ w_in.astype(f32)
    q = proj[..., 0 * MIX_A:1 * MIX_A]
    fg = proj[..., 1 * MIX_A:2 * MIX_A]
    iv = proj[..., 2 * MIX_A:3 * MIX_A]
    og = proj[..., 3 * MIX_A:4 * MIX_A]
    ca = proj[..., 4 * MIX_A:4 * MIX_A + MIX_B]
    cg = proj[..., 4 * MIX_A + MIX_B:]

    def heads(a):
        return a.reshape(B, T, A_HEADS, -1)

    forget = lb + (1.0 - lb) * jax.nn.sigmoid(fg)
    o, s_new = _hgrn2_scan(heads(jax.nn.silu(q)), heads(1.0 - forget), heads(iv),
                           heads(jnp.log(forget)), s0.astype(f32))
    o = _rmsnorm(o, hgrn_out_norm) * jax.nn.silu(heads(og))
    o_a = o.reshape(B, T, MIX_A)

    u = ca * jax.nn.sigmoid(cg)
    u_full = jnp.concatenate([buf0.astype(f32), u], axis=1)
    z = lax.conv_general_dilated(u_full, conv_dw.astype(f32)[:, None, :], window_strides=(1,),
                                 padding='VALID', dimension_numbers=('NWC', 'WIO', 'NWC'),
                                 feature_group_count=MIX_B) + conv_dw_bias.astype(f32)
    mu = jnp.mean(z, axis=-1, keepdims=True)
    var = jnp.mean(jnp.square(z - mu), axis=-1, keepdims=True)
    z = (z - mu) * lax.rsqrt(var + EPS) * conv_ln_gain.astype(f32) + conv_ln_bias.astype(f32)
    o_b = jax.nn.silu(z)
    buf_new = u_full[:, T:, :]

    h = h + jnp.concatenate([o_a, o_b], axis=-1) @ w_out.astype(f32)

    xn2 = _rmsnorm(h, norm_ffn)
    h = h + (jax.nn.silu(xn2 @ w_ffn_gate.astype(f32)) * (xn2 @ w_ffn_up.astype(f32))) @ w_ffn_down.astype(f32)

    gate = jax.nn.sigmoid(_rmsnorm(h, norm_ple) @ w_ple_gate.astype(f32))
    h = h + gate * (p_l.astype(f32) @ w_ple_proj.astype(f32))
    return h, s_new, buf_new


def _trunk(x, p, st_hgrn, st_conv, norm_mix, w_in, lb_logits, hgrn_out_norm, conv_dw, conv_dw_bias,
           conv_ln_gain, conv_ln_bias, w_out, norm_ffn, w_ffn_gate, w_ffn_up, w_ffn_down,
           norm_ple, w_ple_gate, w_ple_proj, norm_final):
    h = x.astype(jnp.float32)
    lb_all = jnp.cumsum(jax.nn.softmax(lb_logits.astype(jnp.float32), axis=0), axis=0)
    new_s, new_c = [], []
    for l in range(DEPTH):
        h, s_l, c_l = _layer(h, p[l], st_hgrn[l], st_conv[l], lb_all[l], norm_mix[l], w_in[l],
                             hgrn_out_norm[l], conv_dw[l], conv_dw_bias[l], conv_ln_gain[l],
                             conv_ln_bias[l], w_out[l], norm_ffn[l], w_ffn_gate[l], w_ffn_up[l],
                             w_ffn_down[l], norm_ple[l], w_ple_gate[l], w_ple_proj[l])
        new_s.append(s_l)
        new_c.append(c_l)
    y = _rmsnorm(h, norm_final).astype(x.dtype)
    return y, jnp.stack(new_s, axis=0), jnp.stack(new_c, axis=0)


def setup_inputs(seed: int = 0) -> dict:
    key = jax.random.key(seed)
    ks = jax.random.split(key, 24)
    f32 = jnp.float32

    def nrm(k, shape, scale):
        return jax.random.normal(k, shape, f32) * scale

    return {
        "x_prompt": nrm(ks[0], (BATCH, SEQ, D_MODEL), 1.0),
        "x_sample": nrm(ks[1], (DEC_BATCH, DEC_SEQ, D_MODEL), 1.0),
        "p_prompt": nrm(ks[2], (DEPTH, BATCH, SEQ, D_PLE), 1.0),
        "p_sample": nrm(ks[3], (DEPTH, DEC_BATCH, DEC_SEQ, D_PLE), 1.0),
        "state_hgrn": nrm(ks[4], (DEPTH, DEC_BATCH, A_HEADS, A_DK, A_DV), 0.5),
        "state_conv": nrm(ks[5], (DEPTH, DEC_BATCH, CONV_W - 1, MIX_B), 0.5),
        "norm_mix": 1.0 + nrm(ks[6], (DEPTH, D_MODEL), 0.02),
        "w_in": nrm(ks[7], (DEPTH, D_MODEL, D_IN), D_MODEL ** -0.5),
        "lb_logits": nrm(ks[8], (DEPTH + 1, MIX_A), 0.1),
        "hgrn_out_norm": 1.0 + nrm(ks[9], (DEPTH, A_DV), 0.02),
        "conv_dw": nrm(ks[10], (DEPTH, CONV_W, MIX_B), CONV_W ** -0.5),
        "conv_dw_bias": nrm(ks[11], (DEPTH, MIX_B), 0.02),
        "conv_ln_gain": 1.0 + nrm(ks[12], (DEPTH, MIX_B), 0.02),
        "conv_ln_bias": nrm(ks[13], (DEPTH, MIX_B), 0.02),
        "w_out": nrm(ks[14], (DEPTH, D_MIX, D_MODEL), D_MIX ** -0.5),
        "norm_ffn": 1.0 + nrm(ks[15], (DEPTH, D_MODEL), 0.02),
        "w_ffn_gate": nrm(ks[16], (DEPTH, D_MODEL, D_FF), D_MODEL ** -0.5),
        "w_ffn_up": nrm(ks[17], (DEPTH, D_MODEL, D_FF), D_MODEL ** -0.5),
        "w_ffn_down": nrm(ks[18], (DEPTH, D_FF, D_MODEL), D_FF ** -0.5),
        "norm_ple": 1.0 + nrm(ks[19], (DEPTH, D_MODEL), 0.02),
        "w_ple_gate": nrm(ks[20], (DEPTH, D_MODEL, D_MODEL), D_MODEL ** -0.5),
        "w_ple_proj": nrm(ks[21], (DEPTH, D_PLE, D_MODEL), D_PLE ** -0.5),
        "norm_final": 1.0 + nrm(ks[22], (D_MODEL,), 0.02),
    }


def reference(x_prompt, x_sample, p_prompt, p_sample, state_hgrn, state_conv, norm_mix, w_in,
              lb_logits, hgrn_out_norm, conv_dw, conv_dw_bias, conv_ln_gain, conv_ln_bias, w_out,
              norm_ffn, w_ffn_gate, w_ffn_up, w_ffn_down, norm_ple, w_ple_gate, w_ple_proj,
              norm_final):
    B = x_prompt.shape[0]
    zero_s = jnp.zeros((DEPTH, B, A_HEADS, A_DK, A_DV), jnp.float32)
    zero_c = jnp.zeros((DEPTH, B, CONV_W - 1, MIX_B), jnp.float32)
    y_prompt, state_hgrn_prompt, state_conv_prompt = _trunk(
        x_prompt, p_prompt, zero_s, zero_c, norm_mix, w_in, lb_logits, hgrn_out_norm, conv_dw,
        conv_dw_bias, conv_ln_gain, conv_ln_bias, w_out, norm_ffn, w_ffn_gate, w_ffn_up,
        w_ffn_down, norm_ple, w_ple_gate, w_ple_proj, norm_final)
    y_sample, state_hgrn_sample, state_conv_sample = _trunk(
        x_sample, p_sample, state_hgrn, state_conv, norm_mix, w_in, lb_logits, hgrn_out_norm,
        conv_dw, conv_dw_bias, conv_ln_gain, conv_ln_bias, w_out, norm_ffn, w_ffn_gate, w_ffn_up,
        w_ffn_down, norm_ple, w_ple_gate, w_ple_proj, norm_final)
    return (y_prompt, y_sample, state_hgrn_prompt, state_conv_prompt, state_hgrn_sample, state_conv_sample)
```

```python
import functools

import jax
import jax.numpy as jnp
from jax import lax
from jax.experimental import pallas as pl
from jax.experimental.pallas import tpu as pltpu

F32 = jnp.float32
BF16 = jnp.bfloat16

D_MODEL = 1024
MIX_A = 512
HEADS = 4
HEAD_DIM = 128
MIX_B = 512
CONV_W = 31
CONV_HIST = CONV_W - 1
D_IN = 4 * MIX_A + 2 * MIX_B
D_FF = 2816
D_PLE = 256
CHUNK = 32
EPS = 1e-6

SUBLANES = 8
HIST_ROWS = 32
FFN_SPLIT = 2

PROMPT_BLOCK = 256
TAIL_BLOCK = 256
SAMPLE_SEQS = 16
VMEM_LIMIT = 56 * 1024 * 1024


def _rms(x, g):
    return x * lax.rsqrt(jnp.mean(x * x, axis=-1, keepdims=True) + EPS) * g


def _dot(a, b):
    return jnp.dot(a, b, preferred_element_type=F32)


def _dot_nt(a, b):
    return lax.dot_general(a, b, (((1,), (1,)), ((), ())), preferred_element_type=F32)


def _dot_tn(a, b):
    return lax.dot_general(a, b, (((0,), (0,)), ((), ())), preferred_element_type=F32)


def _div_pow2(x, d):
    shift = d.bit_length() - 1
    assert d == 1 << shift
    return lax.shift_right_logical(x, shift)


def _split3(x):
    hi = x.astype(BF16)
    r = x - hi.astype(F32)
    mid = r.astype(BF16)
    lo = (r - mid.astype(F32)).astype(BF16)
    return hi, mid, lo


def _chunk_cumsums(logf, chunk):
    n = logf.shape[0]
    r = lax.broadcasted_iota(jnp.int32, (2 * n, n), 0)
    c = lax.broadcasted_iota(jnp.int32, (2 * n, n), 1)
    first = r < n
    rr = jnp.where(first, r, r - n)
    same = _div_pow2(rr, chunk) == _div_pow2(c, chunk)
    sel = same & ((first & (c <= rr)) | (jnp.logical_not(first) & (c > rr)))
    m = jnp.where(sel, 1.0, 0.0).astype(BF16)
    hi, mid, lo = _split3(logf)
    res = _dot(m, hi) + _dot(m, mid) + _dot(m, lo)
    return res[:n], res[n:]


def _lower_bound(lb_logits):
    m = jnp.max(lb_logits, axis=0, keepdims=True)
    e = jnp.exp(lb_logits - m)
    return e[0:1] / jnp.sum(e, axis=0, keepdims=True)


def _hgrn_inputs(q, fg, lb, chunk):
    forget = lb + (1.0 - lb) * jax.nn.sigmoid(fg)
    kk = 1.0 - forget
    b, d = _chunk_cumsums(jnp.log(forget), chunk)
    e_b = jnp.exp(b)
    q_dec = jax.nn.silu(q) * e_b
    k_inv = kk * jnp.exp(-b)
    k_dec = kk * jnp.exp(d)
    return q_dec, k_inv, k_dec, b, e_b


def _head_norm_gate(o, og, onorm):
    outs = []
    for h in range(HEADS):
        oh = o[:, h * HEAD_DIM:(h + 1) * HEAD_DIM]
        outs.append(_rms(oh, onorm))
    return jnp.concatenate(outs, axis=-1) * jax.nn.silu(og)


def _layer_norm_swish(z, gain, bias):
    mu = jnp.mean(z, axis=-1, keepdims=True)
    zc = z - mu
    var = jnp.mean(zc * zc, axis=-1, keepdims=True)
    return jax.nn.silu(zc * lax.rsqrt(var + EPS) * gain + bias)


def _prompt_mixer_kernel(x_ref, w_in_ref, nmix_ref, lbl_ref, onorm_ref, cw_ref, cb_ref, lng_ref,
                         lnb_ref, ocat_ref, sout_ref, cout_ref, st_scr, ubuf_scr, o_scr, *, tb):
    t = pl.program_id(1)

    @pl.when(t == 0)
    def _():
        st_scr[...] = jnp.zeros_like(st_scr)
        ubuf_scr[0:HIST_ROWS, :] = jnp.zeros((HIST_ROWS, MIX_B), F32)

    xn = _rms(x_ref[...], nmix_ref[...]).astype(BF16)

    def proj(i):
        return _dot(xn, w_in_ref[:, i * MIX_A:(i + 1) * MIX_A])

    lb = _lower_bound(lbl_ref[...])
    q_dec, k_inv, k_dec, _, e_b = _hgrn_inputs(proj(0), proj(1), lb, CHUNK)
    q_dec = q_dec.astype(BF16)
    k_inv = k_inv.astype(BF16)
    k_dec = k_dec.astype(BF16)
    v = proj(2).astype(BF16)
    ri = lax.broadcasted_iota(jnp.int32, (CHUNK, CHUNK), 0)
    ci = lax.broadcasted_iota(jnp.int32, (CHUNK, CHUNK), 1)
    causal = ci <= ri
    for c in range(tb // CHUNK):
        rows = slice(c * CHUNK, (c + 1) * CHUNK)
        for h in range(HEADS):
            cols = slice(h * HEAD_DIM, (h + 1) * HEAD_DIM)
            qd, ki, kd, vv = q_dec[rows, cols], k_inv[rows, cols], k_dec[rows, cols], v[rows, cols]
            st = st_scr[h]
            o_inter = _dot_nt(qd, st.astype(BF16))
            scores = jnp.where(causal, _dot_nt(qd, ki), 0.0)
            o_scr[rows, cols] = o_inter + _dot(scores.astype(BF16), vv)
            decay = e_b[(c + 1) * CHUNK - 1:(c + 1) * CHUNK, cols]
            st_scr[h] = decay * st + _dot_tn(vv, kd)
    o_a = _head_norm_gate(o_scr[...], proj(3), onorm_ref[...])

    u = proj(4) * jax.nn.sigmoid(proj(5))
    ubuf_scr[HIST_ROWS:HIST_ROWS + tb, :] = u
    off = HIST_ROWS - CONV_HIST
    zs = []
    for r0 in range(0, tb, CHUNK):
        acc = jnp.broadcast_to(cb_ref[...], (CHUNK, MIX_B))
        for j in range(CONV_W):
            acc = acc + cw_ref[j:j + 1, :] * ubuf_scr[r0 + j + off:r0 + j + off + CHUNK, :]
        zs.append(acc)
    o_b = _layer_norm_swish(jnp.concatenate(zs, axis=0), lng_ref[...], lnb_ref[...])

    ocat_ref[:, 0:MIX_A] = o_a.astype(BF16)
    ocat_ref[:, MIX_A:MIX_A + MIX_B] = o_b.astype(BF16)

    @pl.when(t == pl.num_programs(1) - 1)
    def _():
        for h in range(HEADS):
            sout_ref[h] = st_scr[h].T
        cout_ref[...] = ubuf_scr[tb + off:tb + HIST_ROWS, :]

    ubuf_scr[0:HIST_ROWS, :] = ubuf_scr[tb:tb + HIST_ROWS, :]


def _const_spec(shape):
    zeros = (0,) * len(shape)
    return pl.BlockSpec(shape, lambda *_: zeros, pipeline_mode=pl.Buffered(1))


def _prompt_mixer(x, w_in, nmix, lbl, onorm, cw, cb, lng, lnb):
    bsz, seq, _ = x.shape
    tb = PROMPT_BLOCK
    assert seq % tb == 0 and tb % CHUNK == 0 and seq >= CONV_HIST
    return pl.pallas_call(
        functools.partial(_prompt_mixer_kernel, tb=tb),
        grid=(bsz, seq // tb),
        in_specs=[
            pl.BlockSpec((None, tb, D_MODEL), lambda b, t: (b, t, 0)),
            _const_spec((D_MODEL, D_IN)),
            _const_spec((1, D_MODEL)),
            _const_spec((2, MIX_A)),
            _const_spec((1, HEAD_DIM)),
            _const_spec((CONV_W, MIX_B)),
            _const_spec((1, MIX_B)),
            _const_spec((1, MIX_B)),
            _const_spec((1, MIX_B)),
        ],
        out_specs=[
            pl.BlockSpec((None, tb, D_MODEL), lambda b, t: (b, t, 0)),
            pl.BlockSpec((None, HEADS, HEAD_DIM, HEAD_DIM), lambda b, t: (b, 0, 0, 0)),
            pl.BlockSpec((None, CONV_HIST, MIX_B), lambda b, t: (b, 0, 0)),
        ],
        out_shape=[
            jax.ShapeDtypeStruct((bsz, seq, D_MODEL), BF16),
            jax.ShapeDtypeStruct((bsz, HEADS, HEAD_DIM, HEAD_DIM), F32),
            jax.ShapeDtypeStruct((bsz, CONV_HIST, MIX_B), F32),
        ],
        scratch_shapes=[
            pltpu.VMEM((HEADS, HEAD_DIM, HEAD_DIM), F32),
            pltpu.VMEM((HIST_ROWS + tb, MIX_B), F32),
            pltpu.VMEM((tb, MIX_A), F32),
        ],
        compiler_params=pltpu.CompilerParams(
            dimension_semantics=("arbitrary", "arbitrary"), vmem_limit_bytes=VMEM_LIMIT),
        name="prompt_mixer",
    )(x, w_in, nmix, lbl, onorm, cw, cb, lng, lnb)


def _in_proj_kernel(x_ref, w_in_ref, nmix_ref, proj_ref):
    xn = _rms(x_ref[...], nmix_ref[...]).astype(BF16)
    proj_ref[...] = _dot(xn, w_in_ref[...])


def _in_proj(x, w_in, nmix):
    n = x.shape[0]
    return pl.pallas_call(
        _in_proj_kernel,
        grid=(1,),
        in_specs=[_const_spec((n, D_MODEL)), _const_spec((D_MODEL, D_IN)), _const_spec((1, D_MODEL))],
        out_specs=pl.BlockSpec((n, D_IN), lambda i: (0, 0)),
        out_shape=jax.ShapeDtypeStruct((n, D_IN), F32),
        compiler_params=pltpu.CompilerParams(
            dimension_semantics=("arbitrary",), vmem_limit_bytes=VMEM_LIMIT),
        name="sample_in_proj",
    )(x, w_in, nmix)


def _sample_mixer_kernel(proj_ref, sin_ref, cin_ref, lbl_ref, onorm_ref, cw_ref, cb_ref, lng_ref,
                         lnb_ref, ocat_ref, sout_ref, cout_ref, b_scr, cwin_scr, z_scr, *, nseq, steps):
    rows_per_group = 2 * SUBLANES
    seqs_per_group = rows_per_group // steps
    ntok = nseq * steps

    def col(i):
        return proj_ref[:, i * MIX_A:(i + 1) * MIX_A]

    lb = _lower_bound(lbl_ref[...])
    q_dec, k_inv, k_dec, b, _ = _hgrn_inputs(col(0), col(1), lb, steps)
    q_dec = q_dec.astype(BF16)
    k_inv = k_inv.astype(BF16)
    k_dec = k_dec.astype(BF16)
    v = col(2).astype(BF16)
    pieces = []
    for h in range(HEADS):
        b_scr[h] = b[:, h * HEAD_DIM:(h + 1) * HEAD_DIM]
        pieces.append(b_scr[h, pl.ds(steps - 1, nseq, stride=steps), :])
    pad = HEAD_DIM - HEADS * nseq
    if pad:
        pieces.append(jnp.zeros((pad, HEAD_DIM), F32))
    decay_t = jnp.exp(jnp.concatenate(pieces, axis=0).T)

    ri = lax.broadcasted_iota(jnp.int32, (rows_per_group, rows_per_group), 0)
    ci = lax.broadcasted_iota(jnp.int32, (rows_per_group, rows_per_group), 1)
    causal = (_div_pow2(ri, steps) == _div_pow2(ci, steps)) & (ci <= ri)
    tok = _div_pow2(lax.broadcasted_iota(jnp.int32, (rows_per_group, HEAD_DIM), 0), steps)
    o_groups = []
    for g in range(ntok // rows_per_group):
        rows = slice(g * rows_per_group, (g + 1) * rows_per_group)
        o_heads = []
        for h in range(HEADS):
            cols = slice(h * HEAD_DIM, (h + 1) * HEAD_DIM)
            qd, ki, kd, vv = q_dec[rows, cols], k_inv[rows, cols], k_dec[rows, cols], v[rows, cols]
            scores = jnp.where(causal, _dot_nt(qd, ki), 0.0)
            o = _dot(scores.astype(BF16), vv)
            for i in range(seqs_per_group):
                n = g * seqs_per_group + i
                mine = tok == i
                st = sin_ref[n, h]
                o = o + jnp.where(mine, _dot(qd, st.astype(BF16)), 0.0)
                j = h * nseq + n
                decay = jnp.broadcast_to(decay_t[:, j:j + 1], (HEAD_DIM, HEAD_DIM))
                kd_n = jnp.where(mine, kd, jnp.zeros_like(kd))
                sout_ref[n, h] = decay * st + _dot_tn(kd_n, vv)
            o_heads.append(o)
        o_groups.append(jnp.concatenate(o_heads, axis=-1))
    o_a = _head_norm_gate(jnp.concatenate(o_groups, axis=0), col(3), onorm_ref[...])

    u = col(4) * jax.nn.sigmoid(col(5))
    for n in range(nseq):
        cwin_scr[0:CONV_HIST, :] = cin_ref[n]
        cwin_scr[CONV_HIST:CONV_HIST + steps, :] = u[n * steps:(n + 1) * steps, :]
        acc = jnp.broadcast_to(cb_ref[...], (steps, MIX_B))
        for j in range(CONV_W):
            acc = acc + cw_ref[j:j + 1, :] * cwin_scr[j:j + steps, :]
        z_scr[n * steps:(n + 1) * steps, :] = acc
        cout_ref[n] = cwin_scr[steps:steps + CONV_HIST, :]
    o_b = _layer_norm_swish(z_scr[...], lng_ref[...], lnb_ref[...])

    ocat_ref[:, 0:MIX_A] = o_a.astype(BF16)
    ocat_ref[:, MIX_A:MIX_A + MIX_B] = o_b.astype(BF16)


def _sample_mixer(proj, s_in, c_in, lbl, onorm, cw, cb, lng, lnb, steps):
    nseq_all = s_in.shape[0]
    nseq = SAMPLE_SEQS
    ntok = nseq * steps
    assert nseq_all % nseq == 0 and (2 * SUBLANES) % steps == 0 and HEADS * nseq <= HEAD_DIM
    assert ntok % (2 * SUBLANES) == 0
    return pl.pallas_call(
        functools.partial(_sample_mixer_kernel, nseq=nseq, steps=steps),
        grid=(nseq_all // nseq,),
        in_specs=[
            pl.BlockSpec((ntok, D_IN), lambda i: (i, 0)),
            pl.BlockSpec((nseq, HEADS, HEAD_DIM, HEAD_DIM), lambda i: (i, 0, 0, 0)),
            pl.BlockSpec((nseq, CONV_HIST, MIX_B), lambda i: (i, 0, 0)),
            _const_spec((2, MIX_A)),
            _const_spec((1, HEAD_DIM)),
            _const_spec((CONV_W, MIX_B)),
            _const_spec((1, MIX_B)),
            _const_spec((1, MIX_B)),
            _const_spec((1, MIX_B)),
        ],
        out_specs=[
            pl.BlockSpec((ntok, D_MODEL), lambda i: (i, 0)),
            pl.BlockSpec((nseq, HEADS, HEAD_DIM, HEAD_DIM), lambda i: (i, 0, 0, 0)),
            pl.BlockSpec((nseq, CONV_HIST, MIX_B), lambda i: (i, 0, 0)),
        ],
        out_shape=[
            jax.ShapeDtypeStruct((nseq_all * steps, D_MODEL), BF16),
            jax.ShapeDtypeStruct(s_in.shape, F32),
            jax.ShapeDtypeStruct(c_in.shape, F32),
        ],
        scratch_shapes=[
            pltpu.VMEM((HEADS, ntok, HEAD_DIM), F32),
            pltpu.VMEM((CONV_HIST + 2 * SUBLANES, MIX_B), F32),
            pltpu.VMEM((ntok, MIX_B), F32),
        ],
        compiler_params=pltpu.CompilerParams(
            dimension_semantics=("arbitrary",), vmem_limit_bytes=VMEM_LIMIT),
        name="sample_mixer",
    )(proj, s_in, c_in, lbl, onorm, cw, cb, lng, lnb)


def _tail_kernel(x_ref, ocat_ref, p_ref, w_out_ref, nffn_ref, wg_ref, wu_ref, wd_ref, nple_ref,
                 wpg_ref, wpp_ref, nfin_ref, y_ref):
    h = x_ref[...] + _dot(ocat_ref[...], w_out_ref[...])
    xn = _rms(h, nffn_ref[...]).astype(BF16)
    ffn = None
    width = D_FF // FFN_SPLIT
    for c in range(FFN_SPLIT):
        cols = slice(c * width, (c + 1) * width)
        a = (jax.nn.silu(_dot(xn, wg_ref[:, cols])) * _dot(xn, wu_ref[:, cols])).astype(BF16)
        part = _dot(a, wd_ref[cols, :])
        ffn = part if ffn is None else ffn + part
    h = h + ffn
    gate = jax.nn.sigmoid(_dot(_rms(h, nple_ref[...]).astype(BF16), wpg_ref[...]))
    h = h + gate * _dot(p_ref[...].astype(BF16), wpp_ref[...])
    y_ref[...] = _rms(h, nfin_ref[...])


def _tail(x, ocat, p, w_out, nffn, wg, wu, wd, nple, wpg, wpp, nfin):
    n = x.shape[0]
    tb = min(TAIL_BLOCK, n)
    assert n % tb == 0 and D_FF % (FFN_SPLIT * 128) == 0
    row = lambda width: pl.BlockSpec((tb, width), lambda i: (i, 0))
    return pl.pallas_call(
        _tail_kernel,
        grid=(n // tb,),
        in_specs=[
            row(D_MODEL), row(D_MODEL), row(D_PLE),
            _const_spec((D_MODEL, D_MODEL)),
            _const_spec((1, D_MODEL)),
            _const_spec((D_MODEL, D_FF)),
            _const_spec((D_MODEL, D_FF)),
            _const_spec((D_FF, D_MODEL)),
            _const_spec((1, D_MODEL)),
            _const_spec((D_MODEL, D_MODEL)),
            _const_spec((D_PLE, D_MODEL)),
            _const_spec((1, D_MODEL)),
        ],
        out_specs=row(D_MODEL),
        out_shape=jax.ShapeDtypeStruct((n, D_MODEL), F32),
        compiler_params=pltpu.CompilerParams(
            dimension_semantics=("arbitrary",), vmem_limit_bytes=VMEM_LIMIT),
        name="tail",
    )(x, ocat, p, w_out, nffn, wg, wu, wd, nple, wpg, wpp, nfin)


def kernel(x_prompt, x_sample, p_prompt, p_sample, state_hgrn, state_conv, norm_mix, w_in, lb_logits, hgrn_out_norm, conv_dw, conv_dw_bias, conv_ln_gain, conv_ln_bias, w_out, norm_ffn, w_ffn_gate, w_ffn_up, w_ffn_down, norm_ple, w_ple_gate, w_ple_proj, norm_final):
    depth = w_in.shape[0]
    assert depth == 1
    bsz, seq, _ = x_prompt.shape
    dbsz, dseq, _ = x_sample.shape

    w_in_b = w_in[0].astype(BF16)
    mixer_params = (lb_logits, hgrn_out_norm, conv_dw[0], conv_dw_bias, conv_ln_gain, conv_ln_bias)
    tail_params = (w_out[0].astype(BF16), norm_ffn, w_ffn_gate[0].astype(BF16), w_ffn_up[0].astype(BF16),
                   w_ffn_down[0].astype(BF16), norm_ple, w_ple_gate[0].astype(BF16),
                   w_ple_proj[0].astype(BF16), norm_final.reshape(1, D_MODEL))

    ocat_p, s_p, c_p = _prompt_mixer(x_prompt, w_in_b, norm_mix, *mixer_params)
    y_p = _tail(x_prompt.reshape(bsz * seq, D_MODEL), ocat_p.reshape(bsz * seq, D_MODEL),
                p_prompt.reshape(bsz * seq, D_PLE), *tail_params)

    xs = x_sample.reshape(dbsz * dseq, D_MODEL)
    proj_s = _in_proj(xs, w_in_b, norm_mix)
    ocat_s, s_s, c_s = _sample_mixer(proj_s, state_hgrn[0], state_conv[0], *mixer_params, steps=dseq)
    y_s = _tail(xs, ocat_s, p_sample.reshape(dbsz * dseq, D_PLE), *tail_params)

    return (y_p.reshape(bsz, seq, D_MODEL), y_s.reshape(dbsz, dseq, D_MODEL),
            s_p[None], c_p[None], s_s[None], c_s[None])
```

```python
import functools

import jax
import jax.numpy as jnp
from jax import lax
from jax.experimental import pallas as pl
from jax.experimental.pallas import tpu as pltpu

F32 = jnp.float32
BF16 = jnp.bfloat16

D_MODEL = 1024
MIX_A = 512
HEADS = 4
HEAD_DIM = 128
MIX_B = 512
CONV_W = 31
CONV_HIST = CONV_W - 1
D_IN = 4 * MIX_A + 2 * MIX_B
D_FF = 2816
D_PLE = 256
CHUNK = 32
EPS = 1e-6

SUBLANES = 8
HIST_ROWS = 32
FFN_SPLIT = 2

PROMPT_BLOCK = 256
TAIL_BLOCK = 512
SAMPLE_SEQS = 16
VMEM_LIMIT = 56 * 1024 * 1024


def _rms(x, g):
    return x * lax.rsqrt(jnp.mean(x * x, axis=-1, keepdims=True) + EPS) * g


def _dot(a, b):
    return jnp.dot(a, b, preferred_element_type=F32)


def _dot_nt(a, b):
    return lax.dot_general(a, b, (((1,), (1,)), ((), ())), preferred_element_type=F32)


def _dot_tn(a, b):
    return lax.dot_general(a, b, (((0,), (0,)), ((), ())), preferred_element_type=F32)


def _div_pow2(x, d):
    shift = d.bit_length() - 1
    assert d == 1 << shift
    return lax.shift_right_logical(x, shift)


def _split3(x):
    hi = x.astype(BF16)
    r = x - hi.astype(F32)
    mid = r.astype(BF16)
    lo = (r - mid.astype(F32)).astype(BF16)
    return hi, mid, lo


def _chunk_cumsums(logf, chunk):
    n = logf.shape[0]
    r = lax.broadcasted_iota(jnp.int32, (2 * n, n), 0)
    c = lax.broadcasted_iota(jnp.int32, (2 * n, n), 1)
    first = r < n
    rr = jnp.where(first, r, r - n)
    same = _div_pow2(rr, chunk) == _div_pow2(c, chunk)
    sel = same & ((first & (c <= rr)) | (jnp.logical_not(first) & (c > rr)))
    m = jnp.where(sel, 1.0, 0.0).astype(BF16)
    hi, mid, lo = _split3(logf)
    res = _dot(m, hi) + _dot(m, mid) + _dot(m, lo)
    return res[:n], res[n:]


def _lower_bound(lb_logits):
    m = jnp.max(lb_logits, axis=0, keepdims=True)
    e = jnp.exp(lb_logits - m)
    return e[0:1] / jnp.sum(e, axis=0, keepdims=True)


def _hgrn_inputs(q, fg, lb, chunk):
    forget = lb + (1.0 - lb) * jax.nn.sigmoid(fg)
    kk = 1.0 - forget
    b, d = _chunk_cumsums(jnp.log(forget), chunk)
    e_b = jnp.exp(b)
    q_dec = jax.nn.silu(q) * e_b
    k_inv = kk * jnp.exp(-b)
    k_dec = kk * jnp.exp(d)
    return q_dec, k_inv, k_dec, b, e_b


def _head_norm_gate(o, og, onorm):
    outs = []
    for h in range(HEADS):
        oh = o[:, h * HEAD_DIM:(h + 1) * HEAD_DIM]
        outs.append(_rms(oh, onorm))
    return jnp.concatenate(outs, axis=-1) * jax.nn.silu(og)


def _layer_norm_swish(z, gain, bias):
    mu = jnp.mean(z, axis=-1, keepdims=True)
    zc = z - mu
    var = jnp.mean(zc * zc, axis=-1, keepdims=True)
    return jax.nn.silu(zc * lax.rsqrt(var + EPS) * gain + bias)


def _prompt_mixer_kernel(x_ref, w_in_ref, nmix_ref, lbl_ref, onorm_ref, cw_ref, cb_ref, lng_ref,
                         lnb_ref, ocat_ref, sout_ref, cout_ref, st_scr, ubuf_scr, ushift_scr, o_scr,
                         z_scr, wb_scr, *, tb):
    t = pl.program_id(1)

    @pl.when(t == 0)
    def _():
        st_scr[...] = jnp.zeros_like(st_scr)
        ubuf_scr[0:HIST_ROWS, :] = jnp.zeros((HIST_ROWS, MIX_B), F32)
        for j in range(CONV_W):
            wb_scr[j] = jnp.broadcast_to(cw_ref[j:j + 1, :], (SUBLANES, MIX_B))

    xn = _rms(x_ref[...], nmix_ref[...]).astype(BF16)

    def proj(i):
        return _dot(xn, w_in_ref[:, i * MIX_A:(i + 1) * MIX_A])

    u = proj(4) * jax.nn.sigmoid(proj(5))
    ubuf_scr[HIST_ROWS:HIST_ROWS + tb, :] = u
    off = HIST_ROWS - CONV_HIST
    span = ushift_scr.shape[1]
    for r in range(1, SUBLANES):
        ushift_scr[r - 1] = ubuf_scr[r:r + span, :]

    lb = _lower_bound(lbl_ref[...])
    q_dec, k_inv, k_dec, _, e_b = _hgrn_inputs(proj(0), proj(1), lb, CHUNK)
    q_dec = q_dec.astype(BF16)
    k_inv = k_inv.astype(BF16)
    k_dec = k_dec.astype(BF16)
    v = proj(2).astype(BF16)
    ri = lax.broadcasted_iota(jnp.int32, (tb, tb), 0)
    ci = lax.broadcasted_iota(jnp.int32, (tb, tb), 1)
    causal = (_div_pow2(ri, CHUNK) == _div_pow2(ci, CHUNK)) & (ci <= ri)
    head_cols = [slice(h * HEAD_DIM, (h + 1) * HEAD_DIM) for h in range(HEADS)]
    o_intra = []
    for cols in head_cols:
        scores = jnp.where(causal, _dot_nt(q_dec[:, cols], k_inv[:, cols]), 0.0)
        o_intra.append(_dot(scores.astype(BF16), v[:, cols]))
    incr = [[_dot_tn(v[c * CHUNK:(c + 1) * CHUNK, cols], k_dec[c * CHUNK:(c + 1) * CHUNK, cols])
             for cols in head_cols] for c in range(tb // CHUNK)]
    for c in range(tb // CHUNK):
        rows = slice(c * CHUNK, (c + 1) * CHUNK)
        for h, cols in enumerate(head_cols):
            st = st_scr[h]
            o_scr[rows, cols] = o_intra[h][rows] + _dot_nt(q_dec[rows, cols], st.astype(BF16))
            decay = e_b[(c + 1) * CHUNK - 1:(c + 1) * CHUNK, cols]
            st_scr[h] = decay * st + incr[c][h]
        acc = jnp.broadcast_to(cb_ref[...], (CHUNK, MIX_B))
        for j in range(CONV_W):
            a, r = divmod(j + off, SUBLANES)
            src = ubuf_scr if r == 0 else ushift_scr.at[r - 1]
            lo = c * CHUNK + a * SUBLANES
            wj = jnp.concatenate([wb_scr[j]] * (CHUNK // SUBLANES), axis=0)
            acc = acc + wj * src[lo:lo + CHUNK, :]
        z_scr[rows, :] = acc
    o_a = _head_norm_gate(o_scr[...], proj(3), onorm_ref[...])
    o_b = _layer_norm_swish(z_scr[...], lng_ref[...], lnb_ref[...])

    ocat_ref[:, 0:MIX_A] = o_a.astype(BF16)
    ocat_ref[:, MIX_A:MIX_A + MIX_B] = o_b.astype(BF16)

    @pl.when(t == pl.num_programs(1) - 1)
    def _():
        for h in range(HEADS):
            sout_ref[h] = st_scr[h].T
        cout_ref[...] = ubuf_scr[tb + off:tb + HIST_ROWS, :]

    ubuf_scr[0:HIST_ROWS, :] = ubuf_scr[tb:tb + HIST_ROWS, :]


def _const_spec(shape):
    zeros = (0,) * len(shape)
    return pl.BlockSpec(shape, lambda *_: zeros, pipeline_mode=pl.Buffered(1))


def _prompt_mixer(x, w_in, nmix, lbl, onorm, cw, cb, lng, lnb):
    bsz, seq, _ = x.shape
    tb = PROMPT_BLOCK
    assert seq % tb == 0 and tb % CHUNK == 0 and seq >= CONV_HIST
    return pl.pallas_call(
        functools.partial(_prompt_mixer_kernel, tb=tb),
        grid=(bsz, seq // tb),
        in_specs=[
            pl.BlockSpec((None, tb, D_MODEL), lambda b, t: (b, t, 0)),
            _const_spec((D_MODEL, D_IN)),
            _const_spec((1, D_MODEL)),
            _const_spec((2, MIX_A)),
            _const_spec((1, HEAD_DIM)),
            _const_spec((CONV_W, MIX_B)),
            _const_spec((1, MIX_B)),
            _const_spec((1, MIX_B)),
            _const_spec((1, MIX_B)),
        ],
        out_specs=[
            pl.BlockSpec((None, tb, D_MODEL), lambda b, t: (b, t, 0)),
            pl.BlockSpec((None, HEADS, HEAD_DIM, HEAD_DIM), lambda b, t: (b, 0, 0, 0)),
            pl.BlockSpec((None, CONV_HIST, MIX_B), lambda b, t: (b, 0, 0)),
        ],
        out_shape=[
            jax.ShapeDtypeStruct((bsz, seq, D_MODEL), BF16),
            jax.ShapeDtypeStruct((bsz, HEADS, HEAD_DIM, HEAD_DIM), F32),
            jax.ShapeDtypeStruct((bsz, CONV_HIST, MIX_B), F32),
        ],
        scratch_shapes=[
            pltpu.VMEM((HEADS, HEAD_DIM, HEAD_DIM), F32),
            pltpu.VMEM((HIST_ROWS + tb, MIX_B), F32),
            pltpu.VMEM((SUBLANES - 1, HIST_ROWS - SUBLANES + tb, MIX_B), F32),
            pltpu.VMEM((tb, MIX_A), F32),
            pltpu.VMEM((tb, MIX_B), F32),
            pltpu.VMEM((CONV_W, SUBLANES, MIX_B), F32),
        ],
        compiler_params=pltpu.CompilerParams(
            dimension_semantics=("arbitrary", "arbitrary"), vmem_limit_bytes=VMEM_LIMIT),
        name="prompt_mixer",
    )(x, w_in, nmix, lbl, onorm, cw, cb, lng, lnb)


def _in_proj_kernel(x_ref, w_in_ref, nmix_ref, proj_ref):
    xn = _rms(x_ref[...], nmix_ref[...]).astype(BF16)
    proj_ref[...] = _dot(xn, w_in_ref[...])


def _in_proj(x, w_in, nmix):
    n = x.shape[0]
    return pl.pallas_call(
        _in_proj_kernel,
        grid=(1,),
        in_specs=[_const_spec((n, D_MODEL)), _const_spec((D_MODEL, D_IN)), _const_spec((1, D_MODEL))],
        out_specs=pl.BlockSpec((n, D_IN), lambda i: (0, 0)),
        out_shape=jax.ShapeDtypeStruct((n, D_IN), F32),
        compiler_params=pltpu.CompilerParams(
            dimension_semantics=("arbitrary",), vmem_limit_bytes=VMEM_LIMIT),
        name="sample_in_proj",
    )(x, w_in, nmix)


def _sample_mixer_kernel(proj_ref, sin_ref, cin_ref, lbl_ref, onorm_ref, cw_ref, cb_ref, lng_ref,
                         lnb_ref, ocat_ref, sout_ref, cout_ref, b_scr, cwin_scr, z_scr, *, nseq, steps):
    rows_per_group = 2 * SUBLANES
    seqs_per_group = rows_per_group // steps
    ntok = nseq * steps

    def col(i):
        return proj_ref[:, i * MIX_A:(i + 1) * MIX_A]

    lb = _lower_bound(lbl_ref[...])
    q_dec, k_inv, k_dec, b, _ = _hgrn_inputs(col(0), col(1), lb, steps)
    q_dec = q_dec.astype(BF16)
    k_inv = k_inv.astype(BF16)
    k_dec = k_dec.astype(BF16)
    v = col(2).astype(BF16)
    pieces = []
    for h in range(HEADS):
        b_scr[h] = b[:, h * HEAD_DIM:(h + 1) * HEAD_DIM]
        pieces.append(b_scr[h, pl.ds(steps - 1, nseq, stride=steps), :])
    pad = HEAD_DIM - HEADS * nseq
    if pad:
        pieces.append(jnp.zeros((pad, HEAD_DIM), F32))
    decay_t = jnp.exp(jnp.concatenate(pieces, axis=0).T)

    ri = lax.broadcasted_iota(jnp.int32, (rows_per_group, rows_per_group), 0)
    ci = lax.broadcasted_iota(jnp.int32, (rows_per_group, rows_per_group), 1)
    causal = (_div_pow2(ri, steps) == _div_pow2(ci, steps)) & (ci <= ri)
    tok = _div_pow2(lax.broadcasted_iota(jnp.int32, (rows_per_group, HEAD_DIM), 0), steps)
    o_groups = []
    for g in range(ntok // rows_per_group):
        rows = slice(g * rows_per_group, (g + 1) * rows_per_group)
        o_heads = []
        for h in range(HEADS):
            cols = slice(h * HEAD_DIM, (h + 1) * HEAD_DIM)
            qd, ki, kd, vv = q_dec[rows, cols], k_inv[rows, cols], k_dec[rows, cols], v[rows, cols]
            scores = jnp.where(causal, _dot_nt(qd, ki), 0.0)
            o = _dot(scores.astype(BF16), vv)
            for i in range(seqs_per_group):
                n = g * seqs_per_group + i
                mine = tok == i
                st = sin_ref[n, h]
                o = o + jnp.where(mine, _dot(qd, st.astype(BF16)), 0.0)
                j = h * nseq + n
                decay = jnp.broadcast_to(decay_t[:, j:j + 1], (HEAD_DIM, HEAD_DIM))
                kd_n = jnp.where(mine, kd, jnp.zeros_like(kd))
                sout_ref[n, h] = decay * st + _dot_tn(kd_n, vv)
            o_heads.append(o)
        o_groups.append(jnp.concatenate(o_heads, axis=-1))
    o_a = _head_norm_gate(jnp.concatenate(o_groups, axis=0), col(3), onorm_ref[...])

    u = col(4) * jax.nn.sigmoid(col(5))
    for n in range(nseq):
        cwin_scr[0:CONV_HIST, :] = cin_ref[n]
        cwin_scr[CONV_HIST:CONV_HIST + steps, :] = u[n * steps:(n + 1) * steps, :]
        acc = jnp.broadcast_to(cb_ref[...], (steps, MIX_B))
        for j in range(CONV_W):
            acc = acc + cw_ref[j:j + 1, :] * cwin_scr[j:j + steps, :]
        z_scr[n * steps:(n + 1) * steps, :] = acc
        cout_ref[n] = cwin_scr[steps:steps + CONV_HIST, :]
    o_b = _layer_norm_swish(z_scr[...], lng_ref[...], lnb_ref[...])

    ocat_ref[:, 0:MIX_A] = o_a.astype(BF16)
    ocat_ref[:, MIX_A:MIX_A + MIX_B] = o_b.astype(BF16)


def _sample_mixer(proj, s_in, c_in, lbl, onorm, cw, cb, lng, lnb, steps):
    nseq_all = s_in.shape[0]
    nseq = SAMPLE_SEQS
    ntok = nseq * steps
    assert nseq_all % nseq == 0 and (2 * SUBLANES) % steps == 0 and HEADS * nseq <= HEAD_DIM
    assert ntok % (2 * SUBLANES) == 0
    return pl.pallas_call(
        functools.partial(_sample_mixer_kernel, nseq=nseq, steps=steps),
        grid=(nseq_all // nseq,),
        in_specs=[
            pl.BlockSpec((ntok, D_IN), lambda i: (i, 0)),
            pl.BlockSpec((nseq, HEADS, HEAD_DIM, HEAD_DIM), lambda i: (i, 0, 0, 0)),
            pl.BlockSpec((nseq, CONV_HIST, MIX_B), lambda i: (i, 0, 0)),
            _const_spec((2, MIX_A)),
            _const_spec((1, HEAD_DIM)),
            _const_spec((CONV_W, MIX_B)),
            _const_spec((1, MIX_B)),
            _const_spec((1, MIX_B)),
            _const_spec((1, MIX_B)),
        ],
        out_specs=[
            pl.BlockSpec((ntok, D_MODEL), lambda i: (i, 0)),
            pl.BlockSpec((nseq, HEADS, HEAD_DIM, HEAD_DIM), lambda i: (i, 0, 0, 0)),
            pl.BlockSpec((nseq, CONV_HIST, MIX_B), lambda i: (i, 0, 0)),
        ],
        out_shape=[
            jax.ShapeDtypeStruct((nseq_all * steps, D_MODEL), BF16),
            jax.ShapeDtypeStruct(s_in.shape, F32),
            jax.ShapeDtypeStruct(c_in.shape, F32),
        ],
        scratch_shapes=[
            pltpu.VMEM((HEADS, ntok, HEAD_DIM), F32),
            pltpu.VMEM((CONV_HIST + 2 * SUBLANES, MIX_B), F32),
            pltpu.VMEM((ntok, MIX_B), F32),
        ],
        compiler_params=pltpu.CompilerParams(
            dimension_semantics=("arbitrary",), vmem_limit_bytes=VMEM_LIMIT),
        name="sample_mixer",
    )(proj, s_in, c_in, lbl, onorm, cw, cb, lng, lnb)


def _tail_kernel(x_ref, ocat_ref, p_ref, w_out_ref, nffn_ref, wg_ref, wu_ref, wd_ref, nple_ref,
                 wpg_ref, wpp_ref, nfin_ref, y_ref):
    h = x_ref[...] + _dot(ocat_ref[...], w_out_ref[...])
    xn = _rms(h, nffn_ref[...]).astype(BF16)
    ffn = None
    width = D_FF // FFN_SPLIT
    for c in range(FFN_SPLIT):
        cols = slice(c * width, (c + 1) * width)
        a = (jax.nn.silu(_dot(xn, wg_ref[:, cols])) * _dot(xn, wu_ref[:, cols])).astype(BF16)
        part = _dot(a, wd_ref[cols, :])
        ffn = part if ffn is None else ffn + part
    h = h + ffn
    gate = jax.nn.sigmoid(_dot(_rms(h, nple_ref[...]).astype(BF16), wpg_ref[...]))
    h = h + gate * _dot(p_ref[...].astype(BF16), wpp_ref[...])
    y_ref[...] = _rms(h, nfin_ref[...])


def _tail(x, ocat, p, w_out, nffn, wg, wu, wd, nple, wpg, wpp, nfin):
    n = x.shape[0]
    tb = min(TAIL_BLOCK, n)
    assert n % tb == 0 and D_FF % (FFN_SPLIT * 128) == 0
    row = lambda width: pl.BlockSpec((tb, width), lambda i: (i, 0))
    return pl.pallas_call(
        _tail_kernel,
        grid=(n // tb,),
        in_specs=[
            row(D_MODEL), row(D_MODEL), row(D_PLE),
            _const_spec((D_MODEL, D_MODEL)),
            _const_spec((1, D_MODEL)),
            _const_spec((D_MODEL, D_FF)),
            _const_spec((D_MODEL, D_FF)),
            _const_spec((D_FF, D_MODEL)),
            _const_spec((1, D_MODEL)),
            _const_spec((D_MODEL, D_MODEL)),
            _const_spec((D_PLE, D_MODEL)),
            _const_spec((1, D_MODEL)),
        ],
        out_specs=row(D_MODEL),
        out_shape=jax.ShapeDtypeStruct((n, D_MODEL), F32),
        compiler_params=pltpu.CompilerParams(
            dimension_semantics=("arbitrary",), vmem_limit_bytes=VMEM_LIMIT),
        name="tail",
    )(x, ocat, p, w_out, nffn, wg, wu, wd, nple, wpg, wpp, nfin)


def kernel(x_prompt, x_sample, p_prompt, p_sample, state_hgrn, state_conv, norm_mix, w_in, lb_logits, hgrn_out_norm, conv_dw, conv_dw_bias, conv_ln_gain, conv_ln_bias, w_out, norm_ffn, w_ffn_gate, w_ffn_up, w_ffn_down, norm_ple, w_ple_gate, w_ple_proj, norm_final):
    depth = w_in.shape[0]
    assert depth == 1
    bsz, seq, _ = x_prompt.shape
    dbsz, dseq, _ = x_sample.shape

    w_in_b = w_in[0].astype(BF16)
    mixer_params = (lb_logits, hgrn_out_norm, conv_dw[0], conv_dw_bias, conv_ln_gain, conv_ln_bias)
    tail_params = (w_out[0].astype(BF16), norm_ffn, w_ffn_gate[0].astype(BF16), w_ffn_up[0].astype(BF16),
                   w_ffn_down[0].astype(BF16), norm_ple, w_ple_gate[0].astype(BF16),
                   w_ple_proj[0].astype(BF16), norm_final.reshape(1, D_MODEL))

    ocat_p, s_p, c_p = _prompt_mixer(x_prompt, w_in_b, norm_mix, *mixer_params)
    y_p = _tail(x_prompt.reshape(bsz * seq, D_MODEL), ocat_p.reshape(bsz * seq, D_MODEL),
                p_prompt.reshape(bsz * seq, D_PLE), *tail_params)

    xs = x_sample.reshape(dbsz * dseq, D_MODEL)
    proj_s = _in_proj(xs, w_in_b, norm_mix)
    ocat_s, s_s, c_s = _sample_mixer(proj_s, state_hgrn[0], state_conv[0], *mixer_params, steps=dseq)
    y_s = _tail(xs, ocat_s, p_sample.reshape(dbsz * dseq, D_PLE), *tail_params)

    return (y_p.reshape(bsz, seq, D_MODEL), y_s.reshape(dbsz, dseq, D_MODEL),
            s_p[None], c_p[None], s_s[None], c_s[None])
```
